```python
import math, functools
import jax, jax.numpy as jnp
from jax import lax
import numpy as np

D_MODEL = 4096
BATCH = 32
SEQ = 256
DEPTH = 2
DEC_BATCH = 8
DEC_SEQ = 1024
PAST_LEN = 512

GRID_W = 64
D_ATTN = D_MODEL // 4
N_HEADS_A = 8
HEAD_DIM_A = D_ATTN // N_HEADS_A
NA_KH = 8
NA_KW = 16
Q_BLOCK = 128
D_SSD = D_MODEL // 2
SSD_HEAD_DIM = 64
N_HEADS_S = D_SSD // SSD_HEAD_DIM
SSD_GROUPS = 4
SSD_STATE = 128
SSD_CONV = 5
SSD_CHUNK = 128
D_XBC = D_SSD + 2 * SSD_GROUPS * SSD_STATE
D_POOL = D_MODEL - D_ATTN - D_SSD
POOL_WINDOWS = (2, 4, 8, 16)
POOL_GROUP = D_POOL // len(POOL_WINDOWS)
D_MIX = D_ATTN + D_SSD + D_POOL
D_IN = 3 * D_ATTN + D_SSD + D_XBC + 2 * N_HEADS_S + D_POOL
D_FF = ((8 * D_MODEL // 3 + 255) // 256) * 256
N_MOD = 9
EPS = 1e-6
NEG = -1e30

kernel_name = 'hybrid_flow_na_ssd_pool_step'


def rmsnorm(x, w):
    x32 = x.astype(jnp.float32)
    y = x32 * lax.rsqrt(jnp.mean(x32 * x32, axis=-1, keepdims=True) + EPS)
    return (y * w.astype(jnp.float32)).astype(x.dtype)


def modulate(x, w, shift, scale):
    return rmsnorm(x, w) * (1 + scale[:, None, :]) + shift[:, None, :]


def swiglu(h, wg, wu, wd):
    return (jax.nn.silu(h @ wg) * (h @ wu)) @ wd


def adaln(cvec, w, b):
    return jnp.split(jax.nn.silu(cvec) @ w + b, N_MOD, axis=-1)


def ctx_attention(q, k, v):
    b, L, H, Dh = q.shape
    nb = L // Q_BLOCK
    qb = jnp.moveaxis(q.reshape(b, nb, Q_BLOCK, H, Dh), 1, 0) * (Dh ** -0.5)

    def block(qi):
        s = jnp.einsum('bqhd,bkhd->bhqk', qi, k).astype(jnp.float32)
        p = jax.nn.softmax(s, axis=-1).astype(v.dtype)
        return jnp.einsum('bhqk,bkhd->bqhd', p, v)

    o = lax.map(block, qb)
    return jnp.moveaxis(o, 0, 1).reshape(b, L, H * Dh)


def na_attention(q, k, v, kc, vc, rpb):
    b, L, H, Dh = q.shape
    rows = L // GRID_W
    kh = min(NA_KH, rows)
    qg = q.reshape(b, rows, GRID_W, H, Dh) * (Dh ** -0.5)
    kg = k.reshape(b, rows, GRID_W, H, Dh)
    vg = v.reshape(b, rows, GRID_W, H, Dh)
    col = jnp.arange(GRID_W)
    col_start = jnp.clip(col - NA_KW // 2, 0, GRID_W - NA_KW)
    col_ok = (col[None, :] >= col_start[:, None]) & (col[None, :] < col_start[:, None] + NA_KW)
    dc_idx = jnp.clip(col[None, :] - col[:, None] + NA_KW - 1, 0, 2 * NA_KW - 2)
    rpb32 = rpb.astype(jnp.float32)

    def row_block(r):
        rs = jnp.clip(r - kh // 2, 0, rows - kh)
        k_blk = lax.dynamic_slice_in_dim(kg, rs, kh, axis=1)
        v_blk = lax.dynamic_slice_in_dim(vg, rs, kh, axis=1)
        q_blk = lax.dynamic_index_in_dim(qg, r, axis=1, keepdims=False)
        dr_idx = rs + jnp.arange(kh) - r + NA_KH - 1
        bias = rpb32[:, dr_idx[None, :, None], dc_idx[:, None, :]]
        s_lat = jnp.einsum('bqhd,bikhd->bhqik', q_blk, k_blk).astype(jnp.float32) + bias[None]
        s_lat = jnp.where(col_ok[:, None, :], s_lat, NEG)
        s_ctx = jnp.einsum('bqhd,bkhd->bhqk', q_blk, kc).astype(jnp.float32)
        s = jnp.concatenate([s_lat.reshape(b, H, GRID_W, kh * GRID_W), s_ctx], axis=-1)
        p = jax.nn.softmax(s, axis=-1).astype(v.dtype)
        p_lat = p[..., :kh * GRID_W].reshape(b, H, GRID_W, kh, GRID_W)
        p_ctx = p[..., kh * GRID_W:]
        return (jnp.einsum('bhqik,bikhd->bqhd', p_lat, v_blk)
                + jnp.einsum('bhqk,bkhd->bqhd', p_ctx, vc))

    o = lax.map(row_block, jnp.arange(rows))
    return jnp.moveaxis(o, 0, 1).reshape(b, L, H * Dh)


def dwconv(x, w, bias):
    C = x.shape[-1]
    y = lax.conv_general_dilated(x, w[:, None, :].astype(x.dtype), window_strides=(1,),
                                 padding=[(SSD_CONV // 2, SSD_CONV // 2)],
                                 dimension_numbers=('NWC', 'WIO', 'NWC'), feature_group_count=C)
    return y + bias


def ssd_scan(x, dt, a, bm, cm, h0):
    b, L, H, P = x.shape
    G, N = bm.shape[-2], bm.shape[-1]
    R = H // G
    nc = L // SSD_CHUNK
    f32 = jnp.float32
    xc = x.astype(f32).reshape(b, nc, SSD_CHUNK, G, R, P)
    dtc = dt.astype(f32).reshape(b, nc, SSD_CHUNK, G, R)
    bc = bm.astype(f32).reshape(b, nc, SSD_CHUNK, G, N)
    cc = cm.astype(f32).reshape(b, nc, SSD_CHUNK, G, N)
    da_cs = jnp.cumsum(dtc * a.astype(f32).reshape(G, R), axis=2)
    xdt = xc * dtc[..., None]
    causal = jnp.tril(jnp.ones((SSD_CHUNK, SSD_CHUNK), dtype=bool))
    seg = da_cs[:, :, :, None] - da_cs[:, :, None, :]
    decay = jnp.exp(jnp.where(causal[:, :, None, None], seg, -jnp.inf))
    cb = jnp.einsum('bclgn,bcsgn->bclsg', cc, bc)
    y_diag = jnp.einsum('bclsgr,bcsgrp->bclgrp', cb[..., None] * decay, xdt)
    decay_end = jnp.exp(da_cs[:, :, -1:] - da_cs)
    chunk_states = jnp.einsum('bclgn,bclgrp->bcgrpn', bc, xdt * decay_end[..., None])
    chunk_decay = jnp.exp(da_cs[:, :, -1])

    def step(h, inp):
        s_c, d_c = inp
        return h * d_c[..., None, None] + s_c, h

    h_init = h0.astype(f32).reshape(b, G, R, P, N)
    h_final, h_prev = lax.scan(step, h_init,
                               (jnp.moveaxis(chunk_states, 1, 0), jnp.moveaxis(chunk_decay, 1, 0)))
    h_prev = jnp.moveaxis(h_prev, 0, 1)
    y_off = jnp.einsum('bclgn,bcgrpn->bclgrp', cc, h_prev) * jnp.exp(da_cs)[..., None]
    y = (y_diag + y_off).reshape(b, L, H, P).astype(x.dtype)
    return y, h_final.reshape(b, H, P, N).astype(x.dtype)


def ssd_mixer(xbc, z, dt_raw, conv_w, conv_b, dt_bias, a_log, d_skip, norm_w, h0_fwd, h0_bwd):
    b, L, _ = xbc.shape
    xbc = jax.nn.silu(dwconv(xbc, conv_w, conv_b))
    gn = SSD_GROUPS * SSD_STATE
    x = xbc[..., :D_SSD].reshape(b, L, N_HEADS_S, SSD_HEAD_DIM)
    bm = xbc[..., D_SSD:D_SSD + gn].reshape(b, L, SSD_GROUPS, SSD_STATE)
    cm = xbc[..., D_SSD + gn:].reshape(b, L, SSD_GROUPS, SSD_STATE)
    dt = jax.nn.softplus(dt_raw.astype(jnp.float32) + dt_bias.astype(jnp.float32))
    a = -jnp.exp(a_log.astype(jnp.float32))
    y_f, s_f = ssd_scan(x, dt[:, :, 0], a[0], bm, cm, h0_fwd)
    flip = lambda t: jnp.flip(t, axis=1)
    y_b, s_b = ssd_scan(flip(x), flip(dt[:, :, 1]), a[1], flip(bm), flip(cm), h0_bwd)
    y = y_f + flip(y_b) + x * d_skip[:, None]
    y = y.reshape(b, L, D_SSD) * jax.nn.silu(z)
    return rmsnorm(y, norm_w), jnp.stack([s_f, s_b], axis=1)


def pool_mixer(u, pool_w, pool_scale):
    b, L, C = u.shape
    u32 = u.astype(jnp.float32)
    cs = jnp.concatenate([jnp.zeros((b, 1, C), jnp.float32), jnp.cumsum(u32, axis=1)], axis=1)
    t = jnp.arange(L)
    outs = []
    for gi, w in enumerate(POOL_WINDOWS):
        sl = slice(gi * POOL_GROUP, (gi + 1) * POOL_GROUP)
        lo = jnp.clip(t - w // 2, 0, L - 1)
        hi = jnp.clip(t + w - w // 2 - 1, 0, L - 1)
        csg = cs[..., sl]
        mean = (csg[:, hi + 1] - csg[:, lo]) / (hi - lo + 1).astype(jnp.float32)[None, :, None]
        d = (mean - u32[..., sl]).astype(u.dtype)
        outs.append(jnp.einsum('blc,cd->bld', d, pool_w[gi]))
    return jnp.concatenate(outs, axis=-1) * pool_scale


def token_mixing(h, p, attn_fn, h0_fwd, h0_bwd):
    b, L, _ = h.shape
    u = h @ p['w_in']
    offs = [D_ATTN, 2 * D_ATTN, 3 * D_ATTN, 3 * D_ATTN + D_SSD, 3 * D_ATTN + D_SSD + D_XBC,
            3 * D_ATTN + D_SSD + D_XBC + 2 * N_HEADS_S]
    q, k, v, z, xbc, dt_raw, pin = jnp.split(u, offs, axis=-1)
    q = q.reshape(b, L, N_HEADS_A, HEAD_DIM_A)
    k = k.reshape(b, L, N_HEADS_A, HEAD_DIM_A)
    v = v.reshape(b, L, N_HEADS_A, HEAD_DIM_A)
    y_attn = attn_fn(q, k, v)
    y_ssd, states = ssd_mixer(xbc, z, dt_raw.reshape(b, L, 2, N_HEADS_S), p['ssd_conv_w'], p['ssd_conv_b'],
                              p['ssd_dt_bias'], p['ssd_a_log'], p['ssd_d'], p['ssd_norm'], h0_fwd, h0_bwd)
    y_pool = pool_mixer(pin, p['pool_w'], p['pool_scale'])
    out = jnp.concatenate([y_attn, y_ssd, y_pool], axis=-1) @ p['w_out']
    return out, k, v, states


def trunk_layer(x, mod, p, attn_fn, h0_fwd, h0_bwd):
    sh1, sc1, g1, sh2, sc2, g2, sh3, sc3, g3 = mod
    h = modulate(x, p['norm_ffn1'], sh1, sc1)
    x = x + 0.5 * g1[:, None, :] * swiglu(h, p['ffn1_w_gate'], p['ffn1_w_up'], p['ffn1_w_down'])
    h = modulate(x, p['norm_mix'], sh2, sc2)
    out, k, v, states = token_mixing(h, p, attn_fn, h0_fwd, h0_bwd)
    x = x + g2[:, None, :] * out
    h = modulate(x, p['norm_ffn2'], sh3, sc3)
    x = x + 0.5 * g3[:, None, :] * swiglu(h, p['ffn2_w_gate'], p['ffn2_w_up'], p['ffn2_w_down'])
    return x, k, v, states


def setup_inputs(seed: int = 0) -> dict:
    key = jax.random.key(seed)
    ks = jax.random.split(key, 40)

    def nrm(i, shape, scale=1.0):
        return scale * jax.random.normal(ks[i], shape, jnp.float32)

    D = D_MODEL
    dt0 = jnp.exp(jax.random.uniform(ks[20], (DEPTH, 2, N_HEADS_S), jnp.float32,
                                     minval=math.log(1e-3), maxval=math.log(1e-1)))
    return {
        'x_prompt': nrm(0, (BATCH, SEQ, D)),
        'x_sample': nrm(1, (DEC_BATCH, DEC_SEQ, D)),
        'cache_k': nrm(2, (DEC_BATCH, DEPTH, PAST_LEN, N_HEADS_A, HEAD_DIM_A)),
        'cache_v': nrm(3, (DEC_BATCH, DEPTH, PAST_LEN, N_HEADS_A, HEAD_DIM_A)),
        'state_ssd': nrm(4, (DEC_BATCH, DEPTH, 2, N_HEADS_S, SSD_HEAD_DIM, SSD_STATE), 0.1),
        'c': nrm(5, (DEC_BATCH, D)),
        'c_ctx': nrm(6, (D,)),
        'w_ada': nrm(7, (DEPTH, D, N_MOD * D), 0.5 * D ** -0.5),
        'b_ada': nrm(8, (DEPTH, N_MOD * D), 0.02),
        'norm_ffn1': 1.0 + nrm(9, (DEPTH, D), 0.01),
        'ffn1_w_gate': nrm(10, (DEPTH, D, D_FF), D ** -0.5),
        'ffn1_w_up': nrm(11, (DEPTH, D, D_FF), D ** -0.5),
        'ffn1_w_down': nrm(12, (DEPTH, D_FF, D), D_FF ** -0.5),
        'norm_mix': 1.0 + nrm(13, (DEPTH, D), 0.01),
        'w_in': nrm(14, (DEPTH, D, D_IN), D ** -0.5),
        'na_rpb': nrm(15, (DEPTH, N_HEADS_A, 2 * NA_KH - 1, 2 * NA_KW - 1), 0.1),
        'ssd_conv_w': nrm(16, (DEPTH, SSD_CONV, D_XBC), SSD_CONV ** -0.5),
        'ssd_conv_b': nrm(17, (DEPTH, D_XBC), 0.02),
        'ssd_dt_bias': dt0 + jnp.log(-jnp.expm1(-dt0)),
        'ssd_a_log': jnp.log(jax.random.uniform(ks[21], (DEPTH, 2, N_HEADS_S), jnp.float32, minval=1.0, maxval=16.0)),
        'ssd_d': 1.0 + nrm(22, (DEPTH, N_HEADS_S), 0.1),
        'ssd_norm': 1.0 + nrm(23, (DEPTH, D_SSD), 0.01),
        'pool_w': nrm(24, (DEPTH, len(POOL_WINDOWS), POOL_GROUP, POOL_GROUP), POOL_GROUP ** -0.5),
        'pool_scale': 1.0 + nrm(25, (DEPTH, D_POOL), 0.1),
        'w_out': nrm(26, (DEPTH, D_MIX, D), D_MIX ** -0.5),
        'norm_ffn2': 1.0 + nrm(27, (DEPTH, D), 0.01),
        'ffn2_w_gate': nrm(28, (DEPTH, D, D_FF), D ** -0.5),
        'ffn2_w_up': nrm(29, (DEPTH, D, D_FF), D ** -0.5),
        'ffn2_w_down': nrm(30, (DEPTH, D_FF, D), D_FF ** -0.5),
        'final_norm': 1.0 + nrm(31, (D,), 0.01),
    }


def reference(x_prompt, x_sample, cache_k, cache_v, state_ssd, c, c_ctx, w_ada, b_ada,
              norm_ffn1, ffn1_w_gate, ffn1_w_up, ffn1_w_down, norm_mix, w_in, na_rpb,
              ssd_conv_w, ssd_conv_b, ssd_dt_bias, ssd_a_log, ssd_d, ssd_norm, pool_w, pool_scale,
              w_out, norm_ffn2, ffn2_w_gate, ffn2_w_up, ffn2_w_down, final_norm):
    xp, xs = x_prompt, x_sample
    h0_ctx = jnp.zeros((x_prompt.shape[0], N_HEADS_S, SSD_HEAD_DIM, SSD_STATE), x_prompt.dtype)
    new_k, new_v, new_s = [], [], []
    for l in range(DEPTH):
        p = {
            'norm_ffn1': norm_ffn1[l], 'ffn1_w_gate': ffn1_w_gate[l], 'ffn1_w_up': ffn1_w_up[l],
            'ffn1_w_down': ffn1_w_down[l], 'norm_mix': norm_mix[l], 'w_in': w_in[l],
            'ssd_conv_w': ssd_conv_w[l], 'ssd_conv_b': ssd_conv_b[l], 'ssd_dt_bias': ssd_dt_bias[l],
            'ssd_a_log': ssd_a_log[l], 'ssd_d': ssd_d[l], 'ssd_norm': ssd_norm[l],
            'pool_w': pool_w[l], 'pool_scale': pool_scale[l], 'w_out': w_out[l],
            'norm_ffn2': norm_ffn2[l], 'ffn2_w_gate': ffn2_w_gate[l], 'ffn2_w_up': ffn2_w_up[l],
            'ffn2_w_down': ffn2_w_down[l],
        }
        mod_ctx = adaln(c_ctx[None, :], w_ada[l], b_ada[l])
        xp, k_l, v_l, s_l = trunk_layer(xp, mod_ctx, p, ctx_attention, h0_ctx, h0_ctx)
        new_k.append(k_l)
        new_v.append(v_l)
        new_s.append(s_l)
        mod_lat = adaln(c, w_ada[l], b_ada[l])
        na = functools.partial(na_attention, kc=cache_k[:, l], vc=cache_v[:, l], rpb=na_rpb[l])
        xs, _, _, _ = trunk_layer(xs, mod_lat, p, na, state_ssd[:, l, 0], state_ssd[:, l, 1])
    y_prompt = rmsnorm(xp, final_norm)
    y_sample = rmsnorm(xs, final_norm)
    new_cache_k = jnp.stack(new_k, axis=1)
    new_cache_v = jnp.stack(new_v, axis=1)
    new_state_ssd = jnp.stack(new_s, axis=1)
    return (y_prompt, y_sample, new_cache_k, new_cache_v, new_state_ssd)
```

```python
import functools
import math

import jax
import jax.numpy as jnp
from jax import lax
from jax.experimental import pallas as pl
from jax.experimental.pallas import tpu as pltpu

D_MODEL = 4096
BATCH = 32
SEQ = 256
DEPTH = 2
DEC_BATCH = 8
DEC_SEQ = 1024
PAST_LEN = 512
GRID_W = 64
D_ATTN = D_MODEL // 4
N_HEADS_A = 8
HEAD_DIM_A = D_ATTN // N_HEADS_A
NA_KH = 8
NA_KW = 16
Q_BLOCK = 128
D_SSD = D_MODEL // 2
SSD_HEAD_DIM = 64
N_HEADS_S = D_SSD // SSD_HEAD_DIM
SSD_GROUPS = 4
SSD_STATE = 128
SSD_CONV = 5
SSD_CHUNK = 128
D_XBC = D_SSD + 2 * SSD_GROUPS * SSD_STATE
D_POOL = D_MODEL - D_ATTN - D_SSD
POOL_WINDOWS = (2, 4, 8, 16)
POOL_GROUP = D_POOL // len(POOL_WINDOWS)
D_MIX = D_ATTN + D_SSD + D_POOL
D_IN = 3 * D_ATTN + D_SSD + D_XBC + 2 * N_HEADS_S + D_POOL
D_FF = ((8 * D_MODEL // 3 + 255) // 256) * 256
N_MOD = 9
EPS = 1e-6
NEG = -1e30

F32 = jnp.float32
BF16 = jnp.bfloat16

T_CTX = BATCH * SEQ
T_LAT = DEC_BATCH * DEC_SEQ
T_ALL = T_CTX + T_LAT
N_GROUPS = 16

V7X_VMEM_BYTES = 64 * 1024 * 1024
MIB = 1024 * 1024

TM = 512
FFN_TF = 512
D_FF_PAD = ((D_FF + FFN_TF - 1) // FFN_TF) * FFN_TF
FFN_NC = 512
NORM_ROWS = 64
IN_TN = 512
D_IN_MAIN = 3 * D_ATTN + D_SSD + D_XBC
D_IN_PAD = ((D_IN + IN_TN - 1) // IN_TN) * IN_TN
OUT_TN = 1024
ADA_TN = 1024


def _group_of_tile(i):
    ctx_tiles = T_CTX // TM
    return jnp.where(i < ctx_tiles, 0, 1 + (i - ctx_tiles) // (DEC_SEQ // TM))


def _mod_spec():
    return pl.BlockSpec((1, N_MOD, D_MODEL), lambda i, j: (_group_of_tile(i), 0, 0))


def _adaln_kernel(c_ref, w_ref, b_ref, o_ref):
    c = c_ref[...]
    s = (c * jax.nn.sigmoid(c)).astype(BF16)
    o_ref[...] = jnp.dot(s, w_ref[...].astype(BF16), preferred_element_type=F32) + b_ref[...]


def _adaln(cvec, w, b):
    n = N_MOD * D_MODEL
    out = pl.pallas_call(
        _adaln_kernel,
        grid=(n // ADA_TN,),
        in_specs=[pl.BlockSpec((N_GROUPS, D_MODEL), lambda j: (0, 0)),
                  pl.BlockSpec((D_MODEL, ADA_TN), lambda j: (0, j)),
                  pl.BlockSpec((1, ADA_TN), lambda j: (0, j))],
        out_specs=pl.BlockSpec((N_GROUPS, ADA_TN), lambda j: (0, j)),
        out_shape=jax.ShapeDtypeStruct((N_GROUPS, n), F32),
        compiler_params=pltpu.CompilerParams(
            dimension_semantics=("arbitrary",),
            vmem_limit_bytes=2 * D_MODEL * ADA_TN * 4 + 16 * MIB),
        name="adaln",
    )(cvec, w, b)
    return out.reshape(N_GROUPS, N_MOD, D_MODEL)


def _norm_modulate(src_ref, h_ref, nw_ref, mod_ref, k_shift, k_scale):
    nw = nw_ref[...]
    shift = mod_ref[0, k_shift:k_shift + 1, :]
    scale1 = 1.0 + mod_ref[0, k_scale:k_scale + 1, :]

    def body(r, carry):
        rows = pl.ds(pl.multiple_of(r * NORM_ROWS, NORM_ROWS), NORM_ROWS)
        xs = src_ref[rows, :]
        ms = jnp.mean(xs * xs, axis=-1, keepdims=True)
        y = xs * lax.rsqrt(ms + EPS) * nw
        h_ref[rows, :] = (y * scale1 + shift).astype(BF16)
        return carry

    lax.fori_loop(0, TM // NORM_ROWS, body, 0)


def _ffn_kernel(mod_ref, nw_ref, x_hbm, wgu_ref, wd_ref, o_ref, h_ref, sem, *, k_shift, k_scale, k_gate):
    i = pl.program_id(0)
    j = pl.program_id(1)

    @pl.when(j == 0)
    def _():
        cp = pltpu.make_async_copy(x_hbm.at[pl.ds(pl.multiple_of(i * TM, TM), TM), :], o_ref, sem)
        cp.start()
        cp.wait()
        _norm_modulate(o_ref, h_ref, nw_ref, mod_ref, k_shift, k_scale)

    gu = jnp.dot(h_ref[...], wgu_ref[0], preferred_element_type=F32)
    g = gu[:, :FFN_TF]
    u = gu[:, FFN_TF:]
    act = (g * jax.nn.sigmoid(g) * u).astype(BF16)
    half_gate = 0.5 * mod_ref[0, k_gate:k_gate + 1, :]
    for n in range(0, D_MODEL, FFN_NC):
        upd = jnp.dot(act, wd_ref[:, n:n + FFN_NC], preferred_element_type=F32)
        o_ref[:, n:n + FFN_NC] += half_gate[:, n:n + FFN_NC] * upd


def _ffn(x, mod, nw, wgu, wd, k0):
    kern = functools.partial(_ffn_kernel, k_shift=k0, k_scale=k0 + 1, k_gate=k0 + 2)
    vmem = (2 * TM * D_MODEL * 4 + TM * D_MODEL * 2 + 2 * D_MODEL * 2 * FFN_TF * 2
            + 2 * FFN_TF * D_MODEL * 2 + 8 * MIB)
    return pl.pallas_call(
        kern,
        grid=(T_ALL // TM, D_FF_PAD // FFN_TF),
        in_specs=[_mod_spec(),
                  pl.BlockSpec((1, D_MODEL), lambda i, j: (0, 0)),
                  pl.BlockSpec(memory_space=pl.ANY),
                  pl.BlockSpec((1, D_MODEL, 2 * FFN_TF), lambda i, j: (j, 0, 0)),
                  pl.BlockSpec((FFN_TF, D_MODEL), lambda i, j: (j, 0))],
        out_specs=pl.BlockSpec((TM, D_MODEL), lambda i, j: (i, 0)),
        out_shape=jax.ShapeDtypeStruct((T_ALL, D_MODEL), F32),
        scratch_shapes=[pltpu.VMEM((TM, D_MODEL), BF16), pltpu.SemaphoreType.DMA(())],
        compiler_params=pltpu.CompilerParams(
            dimension_semantics=("arbitrary", "arbitrary"), vmem_limit_bytes=vmem),
        name="ffn",
    )(mod, nw, x, wgu, wd)


def _inproj_kernel(mod_ref, nw_ref, x_ref, w_ref, o_ref, h_ref, *, k_shift, k_scale):
    @pl.when(pl.program_id(1) == 0)
    def _():
        _norm_modulate(x_ref, h_ref, nw_ref, mod_ref, k_shift, k_scale)

    o_ref[...] = jnp.dot(h_ref[...], w_ref[...], preferred_element_type=F32)


def _inproj(x, mod, nw, w):
    kern = functools.partial(_inproj_kernel, k_shift=3, k_scale=4)
    vmem = (2 * TM * D_MODEL * 4 + TM * D_MODEL * 2 + 2 * D_MODEL * IN_TN * 2
            + 2 * TM * IN_TN * 4 + 8 * MIB)
    return pl.pallas_call(
        kern,
        grid=(T_ALL // TM, D_IN_PAD // IN_TN),
        in_specs=[_mod_spec(),
                  pl.BlockSpec((1, D_MODEL), lambda i, j: (0, 0)),
                  pl.BlockSpec((TM, D_MODEL), lambda i, j: (i, 0)),
                  pl.BlockSpec((D_MODEL, IN_TN), lambda i, j: (0, j))],
        out_specs=pl.BlockSpec((TM, IN_TN), lambda i, j: (i, j)),
        out_shape=jax.ShapeDtypeStruct((T_ALL, D_IN_PAD), F32),
        scratch_shapes=[pltpu.VMEM((TM, D_MODEL), BF16)],
        compiler_params=pltpu.CompilerParams(
            dimension_semantics=("arbitrary", "arbitrary"), vmem_limit_bytes=vmem),
        name="inproj",
    )(mod, nw, x, w)


def _outproj_kernel(mod_ref, y_ref, x_ref, w_ref, o_ref):
    upd = jnp.dot(y_ref[...], w_ref[...], preferred_element_type=F32)
    o_ref[...] = x_ref[...] + mod_ref[0, 5:6, :] * upd


def _outproj(x, y, mod, w):
    vmem = (2 * TM * D_MIX * 2 + 2 * D_MIX * OUT_TN * 2 + 4 * TM * OUT_TN * 4 + 8 * MIB)
    return pl.pallas_call(
        _outproj_kernel,
        grid=(T_ALL // TM, D_MODEL // OUT_TN),
        in_specs=[pl.BlockSpec((1, N_MOD, OUT_TN), lambda i, j: (_group_of_tile(i), 0, j)),
                  pl.BlockSpec((TM, D_MIX), lambda i, j: (i, 0)),
                  pl.BlockSpec((TM, OUT_TN), lambda i, j: (i, j)),
                  pl.BlockSpec((D_MIX, OUT_TN), lambda i, j: (0, j))],
        out_specs=pl.BlockSpec((TM, OUT_TN), lambda i, j: (i, j)),
        out_shape=jax.ShapeDtypeStruct((T_ALL, D_MODEL), F32),
        compiler_params=pltpu.CompilerParams(
            dimension_semantics=("arbitrary", "arbitrary"), vmem_limit_bytes=vmem),
        name="outproj",
    )(mod, y, x, w)


def _final_norm_kernel(x_ref, w_ref, o_ref):
    xs = x_ref[...]
    ms = jnp.mean(xs * xs, axis=-1, keepdims=True)
    o_ref[...] = xs * lax.rsqrt(ms + EPS) * w_ref[...]


def _final_norm(x, w, row0, rows):
    tr = 256
    off = row0 // tr
    return pl.pallas_call(
        _final_norm_kernel,
        grid=(rows // tr,),
        in_specs=[pl.BlockSpec((tr, D_MODEL), lambda i: (i + off, 0)),
                  pl.BlockSpec((1, D_MODEL), lambda i: (0, 0))],
        out_specs=pl.BlockSpec((tr, D_MODEL), lambda i: (i, 0)),
        out_shape=jax.ShapeDtypeStruct((rows, D_MODEL), F32),
        compiler_params=pltpu.CompilerParams(dimension_semantics=("arbitrary",)),
        name="final_norm",
    )(x, w)


def _rmsnorm(x, w):
    x32 = x.astype(F32)
    y = x32 * lax.rsqrt(jnp.mean(x32 * x32, axis=-1, keepdims=True) + EPS)
    return (y * w.astype(F32)).astype(x.dtype)


def _ctx_attention(q, k, v):
    b, L, H, Dh = q.shape
    nb = L // Q_BLOCK
    qb = jnp.moveaxis(q.reshape(b, nb, Q_BLOCK, H, Dh), 1, 0) * (Dh ** -0.5)

    def block(qi):
        s = jnp.einsum('bqhd,bkhd->bhqk', qi, k).astype(F32)
        p = jax.nn.softmax(s, axis=-1).astype(v.dtype)
        return jnp.einsum('bhqk,bkhd->bqhd', p, v)

    o = lax.map(block, qb)
    return jnp.moveaxis(o, 0, 1).reshape(b, L, H * Dh)


def _na_attention(q, k, v, kc, vc, rpb):
    b, L, H, Dh = q.shape
    rows = L // GRID_W
    kh = min(NA_KH, rows)
    qg = q.reshape(b, rows, GRID_W, H, Dh) * (Dh ** -0.5)
    kg = k.reshape(b, rows, GRID_W, H, Dh)
    vg = v.reshape(b, rows, GRID_W, H, Dh)
    col = jnp.arange(GRID_W)
    col_start = jnp.clip(col - NA_KW // 2, 0, GRID_W - NA_KW)
    col_ok = (col[None, :] >= col_start[:, None]) & (col[None, :] < col_start[:, None] + NA_KW)
    dc_idx = jnp.clip(col[None, :] - col[:, None] + NA_KW - 1, 0, 2 * NA_KW - 2)
    rpb32 = rpb.astype(F32)

    def row_block(r):
        rs = jnp.clip(r - kh // 2, 0, rows - kh)
        k_blk = lax.dynamic_slice_in_dim(kg, rs, kh, axis=1)
        v_blk = lax.dynamic_slice_in_dim(vg, rs, kh, axis=1)
        q_blk = lax.dynamic_index_in_dim(qg, r, axis=1, keepdims=False)
        dr_idx = rs + jnp.arange(kh) - r + NA_KH - 1
        bias = rpb32[:, dr_idx[None, :, None], dc_idx[:, None, :]]
        s_lat = jnp.einsum('bqhd,bikhd->bhqik', q_blk, k_blk).astype(F32) + bias[None]
        s_lat = jnp.where(col_ok[:, None, :], s_lat, NEG)
        s_ctx = jnp.einsum('bqhd,bkhd->bhqk', q_blk, kc).astype(F32)
        s = jnp.concatenate([s_lat.reshape(b, H, GRID_W, kh * GRID_W), s_ctx], axis=-1)
        p = jax.nn.softmax(s, axis=-1).astype(v.dtype)
        p_lat = p[..., :kh * GRID_W].reshape(b, H, GRID_W, kh, GRID_W)
        p_ctx = p[..., kh * GRID_W:]
        return (jnp.einsum('bhqik,bikhd->bqhd', p_lat, v_blk)
                + jnp.einsum('bhqk,bkhd->bqhd', p_ctx, vc))

    o = lax.map(row_block, jnp.arange(rows))
    return jnp.moveaxis(o, 0, 1).reshape(b, L, H * Dh)


def _dwconv(x, w, bias):
    C = x.shape[-1]
    y = lax.conv_general_dilated(x, w[:, None, :].astype(x.dtype), window_strides=(1,),
                                 padding=[(SSD_CONV // 2, SSD_CONV // 2)],
                                 dimension_numbers=('NWC', 'WIO', 'NWC'), feature_group_count=C)
    return y + bias


def _ssd_scan(x, dt, a, bm, cm, h0):
    b, L, H, P = x.shape
    G, N = bm.shape[-2], bm.shape[-1]
    R = H // G
    nc = L // SSD_CHUNK
    xc = x.astype(F32).reshape(b, nc, SSD_CHUNK, G, R, P)
    dtc = dt.astype(F32).reshape(b, nc, SSD_CHUNK, G, R)
    bc = bm.astype(F32).reshape(b, nc, SSD_CHUNK, G, N)
    cc = cm.astype(F32).reshape(b, nc, SSD_CHUNK, G, N)
    da_cs = jnp.cumsum(dtc * a.astype(F32).reshape(G, R), axis=2)
    xdt = xc * dtc[..., None]
    causal = jnp.tril(jnp.ones((SSD_CHUNK, SSD_CHUNK), dtype=bool))
    seg = da_cs[:, :, :, None] - da_cs[:, :, None, :]
    decay = jnp.exp(jnp.where(causal[:, :, None, None], seg, -jnp.inf))
    cb = jnp.einsum('bclgn,bcsgn->bclsg', cc, bc)
    y_diag = jnp.einsum('bclsgr,bcsgrp->bclgrp', cb[..., None] * decay, xdt)
    decay_end = jnp.exp(da_cs[:, :, -1:] - da_cs)
    chunk_states = jnp.einsum('bclgn,bclgrp->bcgrpn', bc, xdt * decay_end[..., None])
    chunk_decay = jnp.exp(da_cs[:, :, -1])

    def step(h, inp):
        s_c, d_c = inp
        return h * d_c[..., None, None] + s_c, h

    h_init = h0.astype(F32).reshape(b, G, R, P, N)
    h_final, h_prev = lax.scan(step, h_init,
                               (jnp.moveaxis(chunk_states, 1, 0), jnp.moveaxis(chunk_decay, 1, 0)))
    h_prev = jnp.moveaxis(h_prev, 0, 1)
    y_off = jnp.einsum('bclgn,bcgrpn->bclgrp', cc, h_prev) * jnp.exp(da_cs)[..., None]
    y = (y_diag + y_off).reshape(b, L, H, P).astype(x.dtype)
    return y, h_final.reshape(b, H, P, N).astype(x.dtype)


def _ssd_mixer(xbc, z, dt_raw, conv_w, conv_b, dt_bias, a_log, d_skip, norm_w, h0_fwd, h0_bwd):
    b, L, _ = xbc.shape
    xbc = jax.nn.silu(_dwconv(xbc, conv_w, conv_b))
    gn = SSD_GROUPS * SSD_STATE
    x = xbc[..., :D_SSD].reshape(b, L, N_HEADS_S, SSD_HEAD_DIM)
    bm = xbc[..., D_SSD:D_SSD + gn].reshape(b, L, SSD_GROUPS, SSD_STATE)
    cm = xbc[..., D_SSD + gn:].reshape(b, L, SSD_GROUPS, SSD_STATE)
    dt = jax.nn.softplus(dt_raw.astype(F32) + dt_bias.astype(F32))
    a = -jnp.exp(a_log.astype(F32))
    y_f, s_f = _ssd_scan(x, dt[:, :, 0], a[0], bm, cm, h0_fwd)
    flip = lambda t: jnp.flip(t, axis=1)
    y_b, s_b = _ssd_scan(flip(x), flip(dt[:, :, 1]), a[1], flip(bm), flip(cm), h0_bwd)
    y = y_f + flip(y_b) + x * d_skip[:, None]
    y = y.reshape(b, L, D_SSD) * jax.nn.silu(z)
    return _rmsnorm(y, norm_w), jnp.stack([s_f, s_b], axis=1)


def _pool_mixer(u, pool_w, pool_scale):
    b, L, C = u.shape
    u32 = u.astype(F32)
    cs = jnp.concatenate([jnp.zeros((b, 1, C), F32), jnp.cumsum(u32, axis=1)], axis=1)
    t = jnp.arange(L)
    outs = []
    for gi, w in enumerate(POOL_WINDOWS):
        sl = slice(gi * POOL_GROUP, (gi + 1) * POOL_GROUP)
        lo = jnp.clip(t - w // 2, 0, L - 1)
        hi = jnp.clip(t + w - w // 2 - 1, 0, L - 1)
        csg = cs[..., sl]
        mean = (csg[:, hi + 1] - csg[:, lo]) / (hi - lo + 1).astype(F32)[None, :, None]
        d = (mean - u32[..., sl]).astype(u.dtype)
        outs.append(jnp.einsum('blc,cd->bld', d, pool_w[gi]))
    return jnp.concatenate(outs, axis=-1) * pool_scale


def _mix_group(u, b, L, attn_fn, p, h0_fwd, h0_bwd):
    u = u.reshape(b, L, D_IN_PAD)
    q = u[..., :D_ATTN].reshape(b, L, N_HEADS_A, HEAD_DIM_A)
    k = u[..., D_ATTN:2 * D_ATTN].reshape(b, L, N_HEADS_A, HEAD_DIM_A)
    v = u[..., 2 * D_ATTN:3 * D_ATTN].reshape(b, L, N_HEADS_A, HEAD_DIM_A)
    z = u[..., 3 * D_ATTN:3 * D_ATTN + D_SSD]
    xbc = u[..., 3 * D_ATTN + D_SSD:D_IN_MAIN]
    pin = u[..., D_IN_MAIN:D_IN_MAIN + D_POOL]
    dt_raw = u[..., D_IN_MAIN + D_POOL:D_IN_MAIN + D_POOL + 2 * N_HEADS_S]
    y_attn = attn_fn(q, k, v)
    y_ssd, states = _ssd_mixer(xbc, z, dt_raw.reshape(b, L, 2, N_HEADS_S), p['ssd_conv_w'], p['ssd_conv_b'],
                               p['ssd_dt_bias'], p['ssd_a_log'], p['ssd_d'], p['ssd_norm'], h0_fwd, h0_bwd)
    y_pool = _pool_mixer(pin, p['pool_w'], p['pool_scale'])
    y = jnp.concatenate([y_attn, y_ssd, y_pool], axis=-1).reshape(b * L, D_MIX)
    return y, k, v, states


def _pack_ffn(wg, wu, wd):
    pad = D_FF_PAD - D_FF
    nt = D_FF_PAD // FFN_TF
    wg = jnp.pad(wg.astype(BF16), ((0, 0), (0, pad))).reshape(D_MODEL, nt, FFN_TF)
    wu = jnp.pad(wu.astype(BF16), ((0, 0), (0, pad))).reshape(D_MODEL, nt, FFN_TF)
    wgu = jnp.transpose(jnp.concatenate([wg, wu], axis=-1), (1, 0, 2))
    wdp = jnp.pad(wd.astype(BF16), ((0, pad), (0, 0)))
    return wgu, wdp


def _pack_w_in(w):
    dt0 = D_IN_MAIN
    dt1 = D_IN_MAIN + 2 * N_HEADS_S
    w = jnp.concatenate([w[:, :dt0], w[:, dt1:], w[:, dt0:dt1]], axis=1).astype(BF16)
    return jnp.pad(w, ((0, 0), (0, D_IN_PAD - D_IN)))


def kernel(x_prompt, x_sample, cache_k, cache_v, state_ssd, c, c_ctx, w_ada, b_ada, norm_ffn1, ffn1_w_gate,
           ffn1_w_up, ffn1_w_down, norm_mix, w_in, na_rpb, ssd_conv_w, ssd_conv_b, ssd_dt_bias, ssd_a_log, ssd_d,
           ssd_norm, pool_w, pool_scale, w_out, norm_ffn2, ffn2_w_gate, ffn2_w_up, ffn2_w_down, final_norm):
    x = jnp.concatenate([x_prompt.reshape(T_CTX, D_MODEL), x_sample.reshape(T_LAT, D_MODEL)], axis=0)
    cvec = jnp.concatenate([c_ctx[None, :], c, jnp.zeros((N_GROUPS - 1 - DEC_BATCH, D_MODEL), F32)], axis=0)
    h0_ctx = jnp.zeros((BATCH, N_HEADS_S, SSD_HEAD_DIM, SSD_STATE), F32)
    new_k, new_v, new_s = [], [], []
    for l in range(DEPTH):
        p = {'ssd_conv_w': ssd_conv_w[l], 'ssd_conv_b': ssd_conv_b[l], 'ssd_dt_bias': ssd_dt_bias[l],
             'ssd_a_log': ssd_a_log[l], 'ssd_d': ssd_d[l], 'ssd_norm': ssd_norm[l],
             'pool_w': pool_w[l], 'pool_scale': pool_scale[l]}
        mod = _adaln(cvec, w_ada[l], b_ada[l][None, :])
        wgu1, wd1 = _pack_ffn(ffn1_w_gate[l], ffn1_w_up[l], ffn1_w_down[l])
        x = _ffn(x, mod, norm_ffn1[l][None, :], wgu1, wd1, 0)
        u = _inproj(x, mod, norm_mix[l][None, :], _pack_w_in(w_in[l]))
        y_c, k_l, v_l, s_l = _mix_group(u[:T_CTX], BATCH, SEQ, _ctx_attention, p, h0_ctx, h0_ctx)
        na = functools.partial(_na_attention, kc=cache_k[:, l], vc=cache_v[:, l], rpb=na_rpb[l])
        y_l, _, _, _ = _mix_group(u[T_CTX:], DEC_BATCH, DEC_SEQ, na, p, state_ssd[:, l, 0], state_ssd[:, l, 1])
        new_k.append(k_l)
        new_v.append(v_l)
        new_s.append(s_l)
        y = jnp.concatenate([y_c, y_l], axis=0).astype(BF16)
        x = _outproj(x, y, mod, w_out[l].astype(BF16))
        wgu2, wd2 = _pack_ffn(ffn2_w_gate[l], ffn2_w_up[l], ffn2_w_down[l])
        x = _ffn(x, mod, norm_ffn2[l][None, :], wgu2, wd2, 6)
    y_prompt = _final_norm(x, final_norm[None, :], 0, T_CTX).reshape(BATCH, SEQ, D_MODEL)
    y_sample = _final_norm(x, final_norm[None, :], T_CTX, T_LAT).reshape(DEC_BATCH, DEC_SEQ, D_MODEL)
    return (y_prompt, y_sample, jnp.stack(new_k, axis=1), jnp.stack(new_v, axis=1), jnp.stack(new_s, axis=1))
```
